```python
import math
import jax, jax.numpy as jnp
from jax import lax
import numpy as np

D_MODEL = 1024
BATCH = 16
SEQ = 4096
DEPTH = 1
DEC_BATCH = 8
DEC_SEQ = 4096
PAST_LEN = 128

D_MIX = D_MODEL
D_LRU = D_MIX // 2
D_POOL = D_MIX - D_LRU
LRU_HEADS = 8
LRU_HEAD_DIM = D_LRU // LRU_HEADS
CONV_WIDTH = 4
CONV_LEFT = 2
RG_C = 8.0
POOL_WINDOWS = (2, 4, 8, 16)
POOL_GROUPS = len(POOL_WINDOWS)
POOL_GROUP_DIM = D_POOL // POOL_GROUPS
D_FF = int(math.ceil((8 * D_MODEL / 3) / 256) * 256)
N_DIR = 2
EPS = 1e-6

kernel_name = "hymba_rglru_pool_encoder"


def rms_norm(x, g):
    xf = x.astype(jnp.float32)
    y = xf * lax.rsqrt(jnp.mean(xf * xf, axis=-1, keepdims=True) + EPS)
    return (y * g.astype(jnp.float32)).astype(x.dtype)


def centred_depthwise_conv(u, w, b):
    S = u.shape[1]
    up = jnp.pad(u, ((0, 0), (CONV_LEFT, CONV_WIDTH - 1 - CONV_LEFT), (0, 0)))
    out = b
    for k in range(CONV_WIDTH):
        out = out + up[:, k:k + S] * w[k]
    return out


def _linear_combine(c1, c2):
    a1, b1 = c1
    a2, b2 = c2
    return a1 * a2, a2 * b1 + b2


def rg_lru_direction(u, w_a, b_a, w_x, b_x, lam, reverse):
    B, S, C = u.shape
    uh = u.reshape(B, S, LRU_HEADS, LRU_HEAD_DIM)
    r = jax.nn.sigmoid(jnp.einsum('bshi,hij->bshj', uh, w_a.astype(jnp.float32)).reshape(B, S, C) + b_a.astype(jnp.float32))
    i = jax.nn.sigmoid(jnp.einsum('bshi,hij->bshj', uh, w_x.astype(jnp.float32)).reshape(B, S, C) + b_x.astype(jnp.float32))
    log_a = -RG_C * r * jax.nn.softplus(-lam.astype(jnp.float32))
    a = jnp.exp(log_a)
    mult = jnp.sqrt(jnp.maximum(-jnp.expm1(2.0 * log_a), 0.0))
    bterm = mult * (i * u)
    _, h = lax.associative_scan(_linear_combine, (a, bterm), axis=1, reverse=reverse)
    return h


def centred_window_mean(u, w):
    S = u.shape[1]
    half = w // 2
    up = jnp.pad(u, ((0, 0), (half, w - half - 1), (0, 0)))
    cs = jnp.pad(lax.cumsum(up, axis=1), ((0, 0), (1, 0), (0, 0)))
    sums = cs[:, w:w + S] - cs[:, :S]
    t = np.arange(S)
    count = (np.minimum(t + w - half, S) - np.maximum(t - half, 0)).astype(np.float32)
    return sums / jnp.asarray(count)[None, :, None]


def pool_mixer(u, pool_w, pool_b, pool_scale):
    B, S, _ = u.shape
    uf = u.astype(jnp.float32)
    groups = []
    for g, w in enumerate(POOL_WINDOWS):
        ug = uf[..., g * POOL_GROUP_DIM:(g + 1) * POOL_GROUP_DIM]
        groups.append(centred_window_mean(ug, w) - ug)
    pg = jnp.stack(groups, axis=2).astype(u.dtype)
    y = jnp.einsum('bsgi,gij->bsgj', pg, pool_w).reshape(B, S, D_POOL) + pool_b
    return y * pool_scale


def token_mixer(h, w_in, conv_w, conv_b, rg_wa, rg_ba, rg_wx, rg_bx, rg_lam, pool_w, pool_b, pool_scale, w_out):
    proj = h @ w_in
    u_lru = proj[..., :D_LRU]
    gate = proj[..., D_LRU:2 * D_LRU]
    u_pool = proj[..., 2 * D_LRU:]
    u = centred_depthwise_conv(u_lru, conv_w, conv_b).astype(jnp.float32)
    h_fwd = rg_lru_direction(u, rg_wa[0], rg_ba[0], rg_wx[0], rg_bx[0], rg_lam[0], reverse=False)
    h_bwd = rg_lru_direction(u, rg_wa[1], rg_ba[1], rg_wx[1], rg_bx[1], rg_lam[1], reverse=True)
    y_lru = ((h_fwd + h_bwd) * jax.nn.gelu(gate.astype(jnp.float32))).astype(h.dtype)
    y_pool = pool_mixer(u_pool, pool_w, pool_b, pool_scale)
    return jnp.concatenate([y_lru, y_pool], axis=-1) @ w_out


def swiglu_ffn(h, w_gate, w_up, w_down):
    return (jax.nn.silu(h @ w_gate) * (h @ w_up)) @ w_down


def encoder_layer(x, l, pre_norm_mix, post_norm_mix, w_in, conv_w, conv_b, rg_wa, rg_ba, rg_wx, rg_bx, rg_lam,
                  pool_w, pool_b, pool_scale, w_out, pre_norm_ffn, post_norm_ffn, w_gate, w_up, w_down):
    h = rms_norm(x, pre_norm_mix[l])
    m = token_mixer(h, w_in[l], conv_w[l], conv_b[l], rg_wa[l], rg_ba[l], rg_wx[l], rg_bx[l], rg_lam[l],
                    pool_w[l], pool_b[l], pool_scale[l], w_out[l])
    x = x + rms_norm(m, post_norm_mix[l])
    h = rms_norm(x, pre_norm_ffn[l])
    f = swiglu_ffn(h, w_gate[l], w_up[l], w_down[l])
    return x + rms_norm(f, post_norm_ffn[l])


def setup_inputs(seed: int = 0) -> dict:
    key = jax.random.key(seed)
    ks = jax.random.split(key, 24)
    f32 = jnp.float32

    def nrm(k, shape, scale):
        return jax.random.normal(k, shape, f32) * scale

    a8 = jax.random.uniform(ks[10], (DEPTH, N_DIR, D_LRU), f32, minval=0.9, maxval=0.999)
    a_base = a8 ** (1.0 / RG_C)
    rg_lam = jnp.log(a_base) - jnp.log1p(-a_base)
    return {
        "x_prompt": nrm(ks[0], (BATCH, SEQ, D_MODEL), 1.0),
        "x_sample": nrm(ks[1], (DEC_BATCH, DEC_SEQ, D_MODEL), 1.0),
        "pre_norm_mix": 1.0 + nrm(ks[2], (DEPTH, D_MODEL), 0.05),
        "post_norm_mix": 1.0 + nrm(ks[3], (DEPTH, D_MODEL), 0.05),
        "w_in": nrm(ks[4], (DEPTH, D_MODEL, 2 * D_LRU + D_POOL), D_MODEL ** -0.5),
        "conv_w": nrm(ks[5], (DEPTH, CONV_WIDTH, D_LRU), CONV_WIDTH ** -0.5),
        "conv_b": nrm(ks[6], (DEPTH, D_LRU), 0.01),
        "rg_wa": nrm(ks[7], (DEPTH, N_DIR, LRU_HEADS, LRU_HEAD_DIM, LRU_HEAD_DIM), LRU_HEAD_DIM ** -0.5),
        "rg_ba": nrm(ks[8], (DEPTH, N_DIR, D_LRU), 0.1),
        "rg_wx": nrm(ks[9], (DEPTH, N_DIR, LRU_HEADS, LRU_HEAD_DIM, LRU_HEAD_DIM), LRU_HEAD_DIM ** -0.5),
        "rg_bx": nrm(ks[11], (DEPTH, N_DIR, D_LRU), 0.1),
        "rg_lam": rg_lam,
        "pool_w": nrm(ks[12], (DEPTH, POOL_GROUPS, POOL_GROUP_DIM, POOL_GROUP_DIM), POOL_GROUP_DIM ** -0.5),
        "pool_b": nrm(ks[13], (DEPTH, D_POOL), 0.01),
        "pool_scale": 1.0 + nrm(ks[14], (DEPTH, D_POOL), 0.1),
        "w_out": nrm(ks[15], (DEPTH, D_MIX, D_MODEL), D_MIX ** -0.5),
        "pre_norm_ffn": 1.0 + nrm(ks[16], (DEPTH, D_MODEL), 0.05),
        "post_norm_ffn": 1.0 + nrm(ks[17], (DEPTH, D_MODEL), 0.05),
        "w_gate": nrm(ks[18], (DEPTH, D_MODEL, D_FF), D_MODEL ** -0.5),
        "w_up": nrm(ks[19], (DEPTH, D_MODEL, D_FF), D_MODEL ** -0.5),
        "w_down": nrm(ks[20], (DEPTH, D_FF, D_MODEL), D_FF ** -0.5),
    }


def reference(x_prompt, x_sample, pre_norm_mix, post_norm_mix, w_in, conv_w, conv_b, rg_wa, rg_ba, rg_wx, rg_bx,
              rg_lam, pool_w, pool_b, pool_scale, w_out, pre_norm_ffn, post_norm_ffn, w_gate, w_up, w_down):
    y_prompt = x_prompt
    y_sample = x_sample
    for l in range(DEPTH):
        y_prompt = encoder_layer(y_prompt, l, pre_norm_mix, post_norm_mix, w_in, conv_w, conv_b, rg_wa, rg_ba,
                                 rg_wx, rg_bx, rg_lam, pool_w, pool_b, pool_scale, w_out, pre_norm_ffn,
                                 post_norm_ffn, w_gate, w_up, w_down)
        y_sample = encoder_layer(y_sample, l, pre_norm_mix, post_norm_mix, w_in, conv_w, conv_b, rg_wa, rg_ba,
                                 rg_wx, rg_bx, rg_lam, pool_w, pool_b, pool_scale, w_out, pre_norm_ffn,
                                 post_norm_ffn, w_gate, w_up, w_down)
    return (y_prompt, y_sample)
```

```python
import functools
import math

import jax
import jax.numpy as jnp
from jax import lax
from jax.experimental import pallas as pl
from jax.experimental.pallas import tpu as pltpu

SUBLANES = 8
LANES = 128
MXU_DIM = 256

CONV_WIDTH = 4
CONV_LEFT = 2
RG_C = 8.0
POOL_WINDOWS = (2, 4, 8, 16)
EPS = 1e-6

TIME_TILE = 64
HALO = 8
FFN_CHUNKS = 2
VMEM_LIMIT_BYTES = 60 * 1024 * 1024


def _rms_norm(x, g):
    return x * lax.rsqrt(jnp.mean(x * x, axis=-1, keepdims=True) + EPS) * g


def _gelu_tanh(x):
    c = math.sqrt(2.0 / math.pi)
    return x * (0.5 * (1.0 + jnp.tanh(c * (x + 0.044715 * (x * x * x)))))


def _softplus(z):
    return jnp.maximum(z, 0.0) + jnp.log1p(jnp.exp(-jnp.abs(z)))


def _block_diag_dot(v_bf16, w_ref):
    return [jnp.dot(v_bf16[:, q * MXU_DIM:(q + 1) * MXU_DIM], w_ref[q], preferred_element_type=jnp.float32)
            for q in range(v_bf16.shape[1] // MXU_DIM)]


def _gather_time_major(dst, row0, src_ref, n_time):
    d = src_ref.shape[-1]
    for j in range(SUBLANES):
        for c in range(d // LANES):
            dst[c, pl.ds(row0 + j, n_time, stride=SUBLANES), :] = src_ref[j, :, c * LANES:(c + 1) * LANES]


def _load_window(xs, xprev_ref, x_ref, xnext_ref, ts):
    _gather_time_major(xs, 0, xprev_ref, HALO)
    _gather_time_major(xs, HALO * SUBLANES, x_ref, ts)
    _gather_time_major(xs, (HALO + ts) * SUBLANES, xnext_ref, HALO)
    return jnp.concatenate([xs[c] for c in range(xs.shape[0])], axis=-1)


def _row_time(n_rows, t_first):
    return t_first + lax.shift_right_logical(lax.broadcasted_iota(jnp.int32, (n_rows, LANES), 0), 3)


def _conv_lru_input(u_lru_win, conv_w_ref, conv_b_ref, m):
    u = conv_b_ref[...]
    for k in range(CONV_WIDTH):
        u = u + u_lru_win[k * SUBLANES:k * SUBLANES + m, :] * conv_w_ref[k:k + 1, :]
    return u


def _lru_coefficients(u, wg_ref, ba_ref, bx_ref, lam_ref):
    zs = _block_diag_dot(u.astype(jnp.bfloat16), wg_ref)
    pre_a = jnp.concatenate([z[:, :MXU_DIM] for z in zs], axis=-1)
    pre_x = jnp.concatenate([z[:, MXU_DIM:] for z in zs], axis=-1)
    r = jax.nn.sigmoid(pre_a + ba_ref[...])
    i = jax.nn.sigmoid(pre_x + bx_ref[...])
    log_a = (-RG_C * r) * _softplus(-lam_ref[...])
    a = jnp.exp(log_a)
    mult = jnp.sqrt(jnp.maximum(1.0 - a * a, 0.0))
    return a, mult * (i * u)


def _scan_tiles(a_scr, b_scr, carry_scr, ts, reverse):
    def body(k, h):
        t = (ts - 1 - k) if reverse else k
        r = pl.multiple_of(t * SUBLANES, SUBLANES)
        h = a_scr[pl.ds(r, SUBLANES), :] * h + b_scr[pl.ds(r, SUBLANES), :]
        b_scr[pl.ds(r, SUBLANES), :] = h
        return h
    carry_scr[...] = lax.fori_loop(0, ts, body, carry_scr[...], unroll=8)


def _masked(v, t_first, seq_len):
    t = _row_time(v.shape[0], t_first)
    valid = (t >= 0) & (t < seq_len)
    return jnp.concatenate([jnp.where(valid, v[:, c * LANES:(c + 1) * LANES], 0.0)
                            for c in range(v.shape[1] // LANES)], axis=-1)


def _bwd_kernel(xprev_ref, x_ref, xnext_ref, g_pre_ref, w_in_ref, conv_w_ref, conv_b_ref, wg_ref, ba_ref,
                bx_ref, lam_ref, hb_ref, xs, a_scr, carry_scr, *, ts, seq_len, d_lru):
    i = pl.program_id(1)
    nt = pl.num_programs(1)
    t0 = (nt - 1 - i) * ts
    m = ts * SUBLANES

    @pl.when(i == 0)
    def _():
        carry_scr[...] = jnp.zeros_like(carry_scr)

    xw = _load_window(xs, xprev_ref, x_ref, xnext_ref, ts)
    lo = (HALO - CONV_LEFT) * SUBLANES
    n_conv = (ts + CONV_WIDTH - 1) * SUBLANES
    xc = xw[lo:lo + n_conv, :]
    h = _rms_norm(xc, g_pre_ref[...]).astype(jnp.bfloat16)
    u_lru = jnp.dot(h, w_in_ref[:, :d_lru], preferred_element_type=jnp.float32)
    u_lru = _masked(u_lru, t0 - CONV_LEFT, seq_len)
    u = _conv_lru_input(u_lru, conv_w_ref, conv_b_ref, m)
    a, b = _lru_coefficients(u, wg_ref, ba_ref, bx_ref, lam_ref)
    a_scr[...] = a
    hb_ref[...] = b
    _scan_tiles(a_scr, hb_ref, carry_scr, ts, reverse=True)


def _fwd_kernel(xprev_ref, x_ref, xnext_ref, hb_ref, g_pre_ref, g_post_ref, w_in_ref, conv_w_ref, conv_b_ref,
                wg_ref, ba_ref, bx_ref, lam_ref, pool_w_ref, pool_b_ref, pool_scale_ref, w_out_ref,
                g_pre_ffn_ref, g_post_ffn_ref, w_gate_ref, w_up_ref, w_down_ref, o_ref,
                xs, a_scr, b_scr, carry_scr, ys, *, ts, seq_len, d_lru, d_pool):
    i = pl.program_id(1)
    t0 = i * ts
    m = ts * SUBLANES
    centre = HALO * SUBLANES

    @pl.when(i == 0)
    def _():
        carry_scr[...] = jnp.zeros_like(carry_scr)

    xw = _load_window(xs, xprev_ref, x_ref, xnext_ref, ts)
    h = _rms_norm(xw, g_pre_ref[...]).astype(jnp.bfloat16)
    proj = jnp.dot(h, w_in_ref[...], preferred_element_type=jnp.float32)
    proj = _masked(proj, t0 - HALO, seq_len)

    lo = (HALO - CONV_LEFT) * SUBLANES
    u_lru = proj[lo:lo + (ts + CONV_WIDTH - 1) * SUBLANES, :d_lru]
    u = _conv_lru_input(u_lru, conv_w_ref, conv_b_ref, m)
    a, b = _lru_coefficients(u, wg_ref, ba_ref, bx_ref, lam_ref)
    a_scr[...] = a
    b_scr[...] = b
    _scan_tiles(a_scr, b_scr, carry_scr, ts, reverse=False)
    gate = proj[centre:centre + m, d_lru:2 * d_lru]
    y_lru = (b_scr[...] + hb_ref[...]) * _gelu_tanh(gate)

    up = proj[:, 2 * d_lru:]
    t_row = _row_time(m, t0)
    groups = []
    gd = d_pool // len(POOL_WINDOWS)
    for gi, w in enumerate(POOL_WINDOWS):
        half = w // 2
        s = up[:, gi * gd:(gi + 1) * gd]
        span = 1
        while span < w:
            s = s[:s.shape[0] - span * SUBLANES, :] + s[span * SUBLANES:, :]
            span *= 2
        first = (HALO - half) * SUBLANES
        sums = s[first:first + m, :]
        count = jnp.minimum(t_row + (w - half), seq_len) - jnp.maximum(t_row - half, 0)
        tok = up[centre:centre + m, gi * gd:(gi + 1) * gd]
        groups.append(sums / count.astype(jnp.float32) - tok)
    pg = jnp.concatenate(groups, axis=-1).astype(jnp.bfloat16)
    y_pool = jnp.concatenate(_block_diag_dot(pg, pool_w_ref), axis=-1)
    y_pool = (y_pool + pool_b_ref[...]) * pool_scale_ref[...]

    mixed = (jnp.dot(y_lru.astype(jnp.bfloat16), w_out_ref[:d_lru, :], preferred_element_type=jnp.float32)
             + jnp.dot(y_pool.astype(jnp.bfloat16), w_out_ref[d_lru:, :], preferred_element_type=jnp.float32))
    x1 = xw[centre:centre + m, :] + _rms_norm(mixed, g_post_ref[...])

    h2 = _rms_norm(x1, g_pre_ffn_ref[...]).astype(jnp.bfloat16)
    d_ff = w_gate_ref.shape[1]
    fc = d_ff // FFN_CHUNKS
    f = None
    for c in range(FFN_CHUNKS):
        gt = jnp.dot(h2, w_gate_ref[:, c * fc:(c + 1) * fc], preferred_element_type=jnp.float32)
        upj = jnp.dot(h2, w_up_ref[:, c * fc:(c + 1) * fc], preferred_element_type=jnp.float32)
        act = (gt * jax.nn.sigmoid(gt) * upj).astype(jnp.bfloat16)
        part = jnp.dot(act, w_down_ref[c * fc:(c + 1) * fc, :], preferred_element_type=jnp.float32)
        f = part if f is None else f + part
    y = x1 + _rms_norm(f, g_post_ffn_ref[...])

    for c in range(ys.shape[0]):
        ys[c] = y[:, c * LANES:(c + 1) * LANES]
    for j in range(SUBLANES):
        for c in range(ys.shape[0]):
            o_ref[j, :, c * LANES:(c + 1) * LANES] = ys[c, pl.ds(j, ts, stride=SUBLANES), :]


def _const_spec(shape):
    nd = len(shape)
    return pl.BlockSpec(shape, lambda g, i: (0,) * nd, pipeline_mode=pl.Buffered(1))


def _pack_block_diag(w):
    n_heads, hd, _ = w.shape
    per = MXU_DIM // hd
    w = w.reshape(n_heads // per, per, hd, hd)
    eye = jnp.eye(per, dtype=w.dtype)
    return jnp.einsum('qpij,pr->qpirj', w, eye).reshape(n_heads // per, MXU_DIM, MXU_DIM)


def _encoder_layer(x, p, ts):
    batch, seq_len, d_model = x.shape
    assert batch % SUBLANES == 0 and seq_len % ts == 0 and ts % HALO == 0 and d_model % LANES == 0
    n_groups, nt = batch // SUBLANES, seq_len // ts
    m = ts * SUBLANES
    d_lru, d_pool = p['d_lru'], p['d_pool']
    assert d_lru % MXU_DIM == 0 and d_pool == len(POOL_WINDOWS) * LANES and max(POOL_WINDOWS) // 2 <= HALO
    halo_blocks = ts // HALO
    last_halo_block = seq_len // HALO - 1
    win_rows = (ts + 2 * HALO) * SUBLANES
    n_slabs = d_model // LANES

    def x_specs(tile_of):
        return [
            pl.BlockSpec((SUBLANES, HALO, d_model),
                         lambda g, i: (g, jnp.maximum(tile_of(i) * halo_blocks - 1, 0), 0)),
            pl.BlockSpec((SUBLANES, ts, d_model), lambda g, i: (g, tile_of(i), 0)),
            pl.BlockSpec((SUBLANES, HALO, d_model),
                         lambda g, i: (g, jnp.minimum((tile_of(i) + 1) * halo_blocks, last_halo_block), 0)),
        ]

    params = pltpu.CompilerParams(dimension_semantics=("arbitrary", "arbitrary"),
                                  vmem_limit_bytes=VMEM_LIMIT_BYTES)
    lru_small = lambda d: [p['wg'][d], p['ba'][d], p['bx'][d], p['lam'][d]]

    bwd_consts = [p['g_pre'], p['w_in'], p['conv_w'], p['conv_b']] + lru_small(1)
    hb = pl.pallas_call(
        functools.partial(_bwd_kernel, ts=ts, seq_len=seq_len, d_lru=d_lru),
        grid=(n_groups, nt),
        in_specs=x_specs(lambda i: nt - 1 - i) + [_const_spec(c.shape) for c in bwd_consts],
        out_specs=pl.BlockSpec((None, m, d_lru), lambda g, i: (g, nt - 1 - i, 0)),
        out_shape=jax.ShapeDtypeStruct((n_groups, seq_len * SUBLANES, d_lru), jnp.float32),
        scratch_shapes=[pltpu.VMEM((n_slabs, win_rows, LANES), jnp.float32),
                        pltpu.VMEM((m, d_lru), jnp.float32),
                        pltpu.VMEM((SUBLANES, d_lru), jnp.float32)],
        compiler_params=params,
        name="lru_backward_state",
    )(x, x, x, *bwd_consts)

    fwd_consts = ([p['g_pre'], p['g_post'], p['w_in'], p['conv_w'], p['conv_b']] + lru_small(0)
                  + [p['pool_w'], p['pool_b'], p['pool_scale'], p['w_out'], p['g_pre_ffn'], p['g_post_ffn'],
                     p['w_gate'], p['w_up'], p['w_down']])
    return pl.pallas_call(
        functools.partial(_fwd_kernel, ts=ts, seq_len=seq_len, d_lru=d_lru, d_pool=d_pool),
        grid=(n_groups, nt),
        in_specs=(x_specs(lambda i: i) + [pl.BlockSpec((None, m, d_lru), lambda g, i: (g, i, 0))]
                  + [_const_spec(c.shape) for c in fwd_consts]),
        out_specs=pl.BlockSpec((SUBLANES, ts, d_model), lambda g, i: (g, i, 0)),
        out_shape=jax.ShapeDtypeStruct(x.shape, x.dtype),
        scratch_shapes=[pltpu.VMEM((n_slabs, win_rows, LANES), jnp.float32),
                        pltpu.VMEM((m, d_lru), jnp.float32),
                        pltpu.VMEM((m, d_lru), jnp.float32),
                        pltpu.VMEM((SUBLANES, d_lru), jnp.float32),
                        pltpu.VMEM((n_slabs, m, LANES), jnp.float32)],
        compiler_params=params,
        name="encoder_forward",
    )(x, x, x, hb, *fwd_consts)


def _layer_params(l, pre_norm_mix, post_norm_mix, w_in, conv_w, conv_b, rg_wa, rg_ba, rg_wx, rg_bx, rg_lam,
                  pool_w, pool_b, pool_scale, w_out, pre_norm_ffn, post_norm_ffn, w_gate, w_up, w_down):
    bf = jnp.bfloat16
    f32 = jnp.float32
    n_dir, n_heads, hd, _ = rg_wa[l].shape
    d_lru = n_heads * hd
    d_pool = pool_b.shape[-1]
    row = lambda v: v.astype(f32).reshape(1, -1)
    wg = [jnp.concatenate([_pack_block_diag(rg_wa[l, d]), _pack_block_diag(rg_wx[l, d])], axis=-1).astype(bf)
          for d in range(n_dir)]
    return dict(
        d_lru=d_lru, d_pool=d_pool,
        g_pre=row(pre_norm_mix[l]), g_post=row(post_norm_mix[l]), w_in=w_in[l].astype(bf),
        conv_w=conv_w[l].astype(f32), conv_b=row(conv_b[l]),
        wg=wg, ba=[row(rg_ba[l, d]) for d in range(n_dir)], bx=[row(rg_bx[l, d]) for d in range(n_dir)],
        lam=[row(rg_lam[l, d]) for d in range(n_dir)],
        pool_w=_pack_block_diag(pool_w[l]).astype(bf), pool_b=row(pool_b[l]), pool_scale=row(pool_scale[l]),
        w_out=w_out[l].astype(bf), g_pre_ffn=row(pre_norm_ffn[l]), g_post_ffn=row(post_norm_ffn[l]),
        w_gate=w_gate[l].astype(bf), w_up=w_up[l].astype(bf), w_down=w_down[l].astype(bf),
    )


def kernel(x_prompt, x_sample, pre_norm_mix, post_norm_mix, w_in, conv_w, conv_b, rg_wa, rg_ba, rg_wx, rg_bx,
           rg_lam, pool_w, pool_b, pool_scale, w_out, pre_norm_ffn, post_norm_ffn, w_gate, w_up, w_down):
    weights = (pre_norm_mix, post_norm_mix, w_in, conv_w, conv_b, rg_wa, rg_ba, rg_wx, rg_bx, rg_lam,
               pool_w, pool_b, pool_scale, w_out, pre_norm_ffn, post_norm_ffn, w_gate, w_up, w_down)
    y_prompt, y_sample = x_prompt, x_sample
    for l in range(w_in.shape[0]):
        p = _layer_params(l, *weights)
        y_prompt = _encoder_layer(y_prompt, p, min(TIME_TILE, y_prompt.shape[1]))
        y_sample = _encoder_layer(y_sample, p, min(TIME_TILE, y_sample.shape[1]))
    return (y_prompt, y_sample)
```
